```python
import math
import jax, jax.numpy as jnp
from jax import lax
import numpy as np

D_MODEL = 2048
BATCH = 2
SEQ = 16384
DEPTH = 2
DEC_BATCH = 8
DEC_SEQ = 32
PAST_LEN = 2048

CHUNK = 64
N_CONV = (DEPTH + 1) // 2
N_ATTN = DEPTH // 2
CONV_E = D_MODEL
CONV_WIDTH = 31
N_HEADS = 8
HEAD_DIM = 128
ATTN_E = N_HEADS * 2 * HEAD_DIM
ROT_DIM = HEAD_DIM // 4
ROPE_THETA = 500000.0
Q_BLOCK = 128
NORM_EPS = 1e-6
SUBLN_EPS = 1e-5
LN_EPS = 1e-5

kernel_name = "hybrid_conformer_diffattn_stream_step"


def rms_norm(x, g, eps=NORM_EPS):
    xf = x.astype(jnp.float32)
    y = xf * lax.rsqrt(jnp.mean(xf * xf, axis=-1, keepdims=True) + eps)
    return (y * g.astype(jnp.float32)).astype(x.dtype)


def layer_norm(x, g, b, eps=LN_EPS):
    xf = x.astype(jnp.float32)
    mu = jnp.mean(xf, axis=-1, keepdims=True)
    var = jnp.mean(jnp.square(xf - mu), axis=-1, keepdims=True)
    y = (xf - mu) * lax.rsqrt(var + eps)
    return (y * g.astype(jnp.float32) + b.astype(jnp.float32)).astype(x.dtype)


def conv_mixer(h, state, w_in, dw, dw_b, ln_g, ln_b, w_out):
    u = h @ w_in
    a, b, gate = jnp.split(u, 3, axis=-1)
    glu = a * jax.nn.sigmoid(b)
    xpad = jnp.concatenate([state.astype(glu.dtype), glu], axis=1)
    y = lax.conv_general_dilated(
        xpad, dw[:, None, :].astype(glu.dtype), window_strides=(1,), padding='VALID',
        dimension_numbers=('NWC', 'WIO', 'NWC'), feature_group_count=CONV_E) + dw_b
    new_state = xpad[:, -(CONV_WIDTH - 1):]
    y = jax.nn.silu(layer_norm(y, ln_g, ln_b)) * jax.nn.silu(gate)
    return y @ w_out, new_state


def rope_partial(x, pos):
    inv_freq = ROPE_THETA ** (-jnp.arange(0, ROT_DIM, 2, dtype=jnp.float32) / ROT_DIM)
    ang = pos.astype(jnp.float32)[:, None] * inv_freq[None, :]
    cos = jnp.cos(ang)[None, :, None, None, :].astype(x.dtype)
    sin = jnp.sin(ang)[None, :, None, None, :].astype(x.dtype)
    x1 = x[..., :ROT_DIM // 2]
    x2 = x[..., ROT_DIM // 2:ROT_DIM]
    rest = x[..., ROT_DIM:]
    return jnp.concatenate([x1 * cos - x2 * sin, x2 * cos + x1 * sin, rest], axis=-1)


def attn_project(h, w_in, pos):
    n, t, _ = h.shape
    u = h @ w_in
    q, k, v, gate = jnp.split(u, 4, axis=-1)
    q = rope_partial(q.reshape(n, t, N_HEADS, 2, HEAD_DIM), pos)
    k = rope_partial(k.reshape(n, t, N_HEADS, 2, HEAD_DIM), pos)
    v = v.reshape(n, t, N_HEADS, 2 * HEAD_DIM)
    return q, k, v, gate


def diff_lambda(lq1, lk1, lq2, lk2, lam_init):
    f32 = jnp.float32
    return (jnp.exp(jnp.sum(lq1.astype(f32) * lk1.astype(f32)))
            - jnp.exp(jnp.sum(lq2.astype(f32) * lk2.astype(f32))) + lam_init)


def diff_attend(q, k, v, lam, mask):
    s = jnp.einsum('nqhcd,nkhcd->nhcqk', q, k).astype(jnp.float32) * (HEAD_DIM ** -0.5)
    if mask is not None:
        s = jnp.where(mask[None, None, None], s, -jnp.inf)
    p = jax.nn.softmax(s, axis=-1)
    a = p[:, :, 0] - lam * p[:, :, 1]
    return jnp.einsum('nhqk,nkhe->nqhe', a, v.astype(jnp.float32))


def attn_finish(o, gate, subln_g, lam_init, w_out):
    n, t = o.shape[0], o.shape[1]
    o = o * lax.rsqrt(jnp.mean(o * o, axis=-1, keepdims=True) + SUBLN_EPS)
    o = o * subln_g.astype(jnp.float32) * (1.0 - lam_init)
    o = o.reshape(n, t, ATTN_E).astype(gate.dtype) * jax.nn.silu(gate)
    return o @ w_out


def prompt_attention(q, k, v, lam):
    n, s = q.shape[0], q.shape[1]
    nb = s // Q_BLOCK
    qb = q.reshape(n, nb, Q_BLOCK, N_HEADS, 2, HEAD_DIM).transpose(1, 0, 2, 3, 4, 5)
    kchunk = jnp.arange(s) // CHUNK

    def one_block(args):
        qblk, idx = args
        qchunk = (idx * Q_BLOCK + jnp.arange(Q_BLOCK)) // CHUNK
        mask = kchunk[None, :] <= qchunk[:, None]
        return diff_attend(qblk, k, v, lam, mask)

    out = lax.map(one_block, (qb, jnp.arange(nb)))
    return out.transpose(1, 0, 2, 3, 4).reshape(n, s, N_HEADS, 2 * HEAD_DIM)


def setup_inputs(seed: int = 0) -> dict:
    key = jax.random.key(seed)
    ks = jax.random.split(key, 24)
    f32 = jnp.float32
    nrm = lambda k, shape, scale: jax.random.normal(k, shape, f32) * scale
    return {
        "x_prompt": nrm(ks[0], (BATCH, SEQ, D_MODEL), 1.0),
        "x_sample": nrm(ks[1], (DEC_BATCH, DEC_SEQ, D_MODEL), 1.0),
        "state_conv": nrm(ks[2], (N_CONV, DEC_BATCH, CONV_WIDTH - 1, CONV_E), 0.5),
        "cache_k": nrm(ks[3], (N_ATTN, DEC_BATCH, PAST_LEN, N_HEADS, 2 * HEAD_DIM), 1.0),
        "cache_v": nrm(ks[4], (N_ATTN, DEC_BATCH, PAST_LEN, N_HEADS, 2 * HEAD_DIM), 1.0),
        "norm_pre": 1.0 + nrm(ks[5], (DEPTH, D_MODEL), 0.02),
        "norm_post": 1.0 + nrm(ks[6], (DEPTH, D_MODEL), 0.02),
        "conv_w_in": nrm(ks[7], (N_CONV, D_MODEL, 3 * CONV_E), D_MODEL ** -0.5),
        "conv_dw": nrm(ks[8], (N_CONV, CONV_WIDTH, CONV_E), CONV_WIDTH ** -0.5),
        "conv_dw_b": nrm(ks[9], (N_CONV, CONV_E), 0.02),
        "conv_ln_g": 1.0 + nrm(ks[10], (N_CONV, CONV_E), 0.02),
        "conv_ln_b": nrm(ks[11], (N_CONV, CONV_E), 0.02),
        "conv_w_out": nrm(ks[12], (N_CONV, CONV_E, D_MODEL), CONV_E ** -0.5),
        "attn_w_in": nrm(ks[13], (N_ATTN, D_MODEL, 4 * ATTN_E), D_MODEL ** -0.5),
        "attn_lq1": nrm(ks[14], (N_ATTN, HEAD_DIM), 0.1),
        "attn_lk1": nrm(ks[15], (N_ATTN, HEAD_DIM), 0.1),
        "attn_lq2": nrm(ks[16], (N_ATTN, HEAD_DIM), 0.1),
        "attn_lk2": nrm(ks[17], (N_ATTN, HEAD_DIM), 0.1),
        "attn_subln_g": 1.0 + nrm(ks[18], (N_ATTN, 2 * HEAD_DIM), 0.02),
        "attn_w_out": nrm(ks[19], (N_ATTN, ATTN_E, D_MODEL), ATTN_E ** -0.5),
    }


def reference(x_prompt, x_sample, state_conv, cache_k, cache_v, norm_pre, norm_post,
              conv_w_in, conv_dw, conv_dw_b, conv_ln_g, conv_ln_b, conv_w_out,
              attn_w_in, attn_lq1, attn_lk1, attn_lq2, attn_lk2, attn_subln_g, attn_w_out):
    yp, ys = x_prompt, x_sample
    n_p, s_p = yp.shape[0], yp.shape[1]
    n_s, s_s = ys.shape[0], ys.shape[1]
    past = cache_k.shape[2]
    conv_p_states, conv_s_states = [], []
    k_p_rows, v_p_rows, k_s_rows, v_s_rows = [], [], [], []

    for i in range(DEPTH):
        j = i // 2
        hp = rms_norm(yp, norm_pre[i])
        hs = rms_norm(ys, norm_pre[i])
        if i % 2 == 0:
            zero_state = jnp.zeros((n_p, CONV_WIDTH - 1, CONV_E), hp.dtype)
            op, st_p = conv_mixer(hp, zero_state, conv_w_in[j], conv_dw[j], conv_dw_b[j],
                                  conv_ln_g[j], conv_ln_b[j], conv_w_out[j])
            os_, st_s = conv_mixer(hs, state_conv[j], conv_w_in[j], conv_dw[j], conv_dw_b[j],
                                   conv_ln_g[j], conv_ln_b[j], conv_w_out[j])
            conv_p_states.append(st_p)
            conv_s_states.append(st_s)
        else:
            lam_init = 0.8 - 0.6 * math.exp(-0.3 * i)
            lam = diff_lambda(attn_lq1[j], attn_lk1[j], attn_lq2[j], attn_lk2[j], lam_init)
            q, k, v, gate = attn_project(hp, attn_w_in[j], jnp.arange(s_p))
            o = prompt_attention(q, k, v, lam)
            op = attn_finish(o, gate, attn_subln_g[j], lam_init, attn_w_out[j])
            k_p_rows.append(k.reshape(n_p, s_p, N_HEADS, 2 * HEAD_DIM))
            v_p_rows.append(v)
            qs, kn, vn, gs = attn_project(hs, attn_w_in[j], past + jnp.arange(s_s))
            k_all = jnp.concatenate(
                [cache_k[j].astype(kn.dtype).reshape(n_s, past, N_HEADS, 2, HEAD_DIM), kn], axis=1)
            v_all = jnp.concatenate([cache_v[j].astype(vn.dtype), vn], axis=1)
            o_s = diff_attend(qs, k_all, v_all, lam, None)
            os_ = attn_finish(o_s, gs, attn_subln_g[j], lam_init, attn_w_out[j])
            k_s_rows.append(kn.reshape(n_s, s_s, N_HEADS, 2 * HEAD_DIM))
            v_s_rows.append(vn)
        yp = yp + rms_norm(op, norm_post[i])
        ys = ys + rms_norm(os_, norm_post[i])

    new_conv_prompt = jnp.stack(conv_p_states)
    new_conv_sample = jnp.stack(conv_s_states)
    new_k_prompt = jnp.stack(k_p_rows)
    new_v_prompt = jnp.stack(v_p_rows)
    new_k_sample = jnp.stack(k_s_rows)
    new_v_sample = jnp.stack(v_s_rows)
    return (yp, ys, new_conv_prompt, new_conv_sample, new_k_prompt, new_v_prompt, new_k_sample, new_v_sample)
```

```python
import functools
import math

import jax
import jax.numpy as jnp
from jax import lax
from jax.experimental import pallas as pl
from jax.experimental.pallas import tpu as pltpu

CHUNK = 64
CONV_WIDTH = 31
N_HEADS = 8
HEAD_DIM = 128
HEAD_W = 2 * HEAD_DIM
ROT_DIM = HEAD_DIM // 4
ROPE_THETA = 500000.0
NORM_EPS = 1e-6
SUBLN_EPS = 1e-5
LN_EPS = 1e-5

F32 = jnp.float32
BF16 = jnp.bfloat16

VMEM_LIMIT = 56 * 1024 * 1024
STATE_ROWS = 32
CONV_ROWS = 64
CONV_LANES = 256
QK_SCALE = HEAD_DIM ** -0.5 * math.log2(math.e)


def _params(sem):
    return pltpu.CompilerParams(dimension_semantics=sem, vmem_limit_bytes=VMEM_LIMIT)


def _rms_rows(x, g):
    ms = jnp.mean(x * x, axis=-1, keepdims=True)
    return x * lax.rsqrt(ms + NORM_EPS) * g


def _silu(x):
    return x * jax.nn.sigmoid(x)


def _inproj_conv_kernel(x_ref, g_ref, wa_ref, wb_ref, wg_ref, glu_ref, gact_ref, h_ref):
    @pl.when(pl.program_id(1) == 0)
    def _():
        h_ref[...] = _rms_rows(x_ref[...], g_ref[...]).astype(BF16)

    h = h_ref[...]
    a = jnp.dot(h, wa_ref[...], preferred_element_type=F32)
    b = jnp.dot(h, wb_ref[...], preferred_element_type=F32)
    gt = jnp.dot(h, wg_ref[...], preferred_element_type=F32)
    glu_ref[...] = a * jax.nn.sigmoid(b)
    gact_ref[...] = _silu(gt).astype(BF16)


def _inproj_conv(x2d, g, w, *, tm, tn):
    m, d = x2d.shape
    e = w.shape[1] // 3
    nj = e // tn
    row = lambda i, j: (i, 0)
    return pl.pallas_call(
        _inproj_conv_kernel,
        grid=(m // tm, nj),
        in_specs=[
            pl.BlockSpec((tm, d), row),
            pl.BlockSpec((1, d), lambda i, j: (0, 0)),
            pl.BlockSpec((d, tn), lambda i, j: (0, j)),
            pl.BlockSpec((d, tn), lambda i, j: (0, j + nj)),
            pl.BlockSpec((d, tn), lambda i, j: (0, j + 2 * nj)),
        ],
        out_specs=[pl.BlockSpec((tm, tn), lambda i, j: (i, j)),
                   pl.BlockSpec((tm, tn), lambda i, j: (i, j))],
        out_shape=[jax.ShapeDtypeStruct((m, e), F32), jax.ShapeDtypeStruct((m, e), BF16)],
        scratch_shapes=[pltpu.VMEM((tm, d), BF16)],
        compiler_params=_params(("parallel", "arbitrary")),
        name="inproj_conv",
    )(x2d, g, w, w, w)


def _conv_taps():
    lead = STATE_ROWS - (CONV_WIDTH - 1)
    groups = {}
    for w in range(CONV_WIDTH):
        o = w + lead
        start = o % 8 if o % 8 >= lead else o % 8 + 8
        groups.setdefault(start, []).append((w, o - start))
    return sorted(groups.items())


def _convmix_kernel(cur_ref, prev_ref, gact_ref, dw_ref, dwb_ref, lng_ref, lnb_ref, z_ref,
                    xp_ref, y_ref, *, tm, zero_first):
    e = cur_ref.shape[-1]
    prev = prev_ref[0]
    if zero_first:
        prev = jnp.where(pl.program_id(1) == 0, 0.0, prev)
    xp_ref[0:STATE_ROWS, :] = prev
    xp_ref[STATE_ROWS:, :] = cur_ref[0]

    rows = min(CONV_ROWS, tm)
    taps = _conv_taps()

    def lane_chunk(c, carry):
        c0 = pl.multiple_of(c * CONV_LANES, CONV_LANES)
        lanes = pl.ds(c0, CONV_LANES)
        for r0 in range(0, tm, rows):
            acc = jnp.broadcast_to(dwb_ref[:, lanes], (rows, CONV_LANES))
            for start, group in taps:
                span = rows + max(off for _, off in group)
                win = xp_ref[pl.ds(r0 + start, span), lanes]
                for w, off in group:
                    acc = acc + win[off:off + rows, :] * dw_ref[pl.ds(w, 1), lanes]
            y_ref[pl.ds(r0, rows), lanes] = acc
        return carry

    lax.fori_loop(0, e // CONV_LANES, lane_chunk, 0)

    y = y_ref[...]
    mu = jnp.mean(y, axis=-1, keepdims=True)
    dlt = y - mu
    var = jnp.mean(dlt * dlt, axis=-1, keepdims=True)
    yn = dlt * lax.rsqrt(var + LN_EPS) * lng_ref[...] + lnb_ref[...]
    z_ref[0] = (_silu(yn) * gact_ref[0].astype(F32)).astype(BF16)


def _convmix(glu, prev, gact, dw, dwb, lng, lnb, *, tm, zero_first):
    n, t, e = glu.shape
    per = tm // STATE_ROWS
    if zero_first:
        prev_map = lambda b, i: (b, jnp.maximum(i * per - 1, 0), 0)
    else:
        prev_map = lambda b, i: (b, 0, 0)
    vec = pl.BlockSpec((1, e), lambda b, i: (0, 0))
    return pl.pallas_call(
        functools.partial(_convmix_kernel, tm=tm, zero_first=zero_first),
        grid=(n, t // tm),
        in_specs=[
            pl.BlockSpec((1, tm, e), lambda b, i: (b, i, 0)),
            pl.BlockSpec((1, STATE_ROWS, e), prev_map),
            pl.BlockSpec((1, tm, e), lambda b, i: (b, i, 0)),
            pl.BlockSpec((CONV_WIDTH, e), lambda b, i: (0, 0)),
            vec, vec, vec,
        ],
        out_specs=pl.BlockSpec((1, tm, e), lambda b, i: (b, i, 0)),
        out_shape=jax.ShapeDtypeStruct((n, t, e), BF16),
        scratch_shapes=[pltpu.VMEM((tm + STATE_ROWS, e), F32), pltpu.VMEM((tm, e), F32)],
        compiler_params=_params(("parallel", "arbitrary")),
        name="convmix",
    )(glu, prev, gact, dw, dwb, lng, lnb)


def _outproj_kernel(z_ref, w_ref, g_ref, res_ref, o_ref):
    u = jnp.dot(z_ref[...], w_ref[...], preferred_element_type=F32)
    o_ref[...] = res_ref[...] + _rms_rows(u, g_ref[...])


def _outproj(z2d, w, g, res2d, *, tm):
    m, k = z2d.shape
    d = w.shape[1]
    return pl.pallas_call(
        _outproj_kernel,
        grid=(m // tm,),
        in_specs=[
            pl.BlockSpec((tm, k), lambda i: (i, 0)),
            pl.BlockSpec((k, d), lambda i: (0, 0)),
            pl.BlockSpec((1, d), lambda i: (0, 0)),
            pl.BlockSpec((tm, d), lambda i: (i, 0)),
        ],
        out_specs=pl.BlockSpec((tm, d), lambda i: (i, 0)),
        out_shape=jax.ShapeDtypeStruct((m, d), F32),
        compiler_params=_params(("parallel",)),
        name="outproj",
    )(z2d, w, g, res2d)


def _rope(x, cos_ref, sina_ref, sinb_ref):
    cos, sina, sinb = cos_ref[...], sina_ref[...], sinb_ref[...]
    half = ROT_DIM // 2
    outs = []
    for c in range(x.shape[1] // HEAD_DIM):
        xc = x[:, c * HEAD_DIM:(c + 1) * HEAD_DIM]
        up = pltpu.roll(xc, HEAD_DIM - half, axis=1)
        dn = pltpu.roll(xc, half, axis=1)
        outs.append(xc * cos + up * sina + dn * sinb)
    return jnp.concatenate(outs, axis=1)


def _inproj_attn_kernel(x_ref, g_ref, wq_ref, wk_ref, wv_ref, wg_ref, cos_ref, sina_ref, sinb_ref,
                        q_ref, k_ref, v_ref, kb_ref, vb_ref, gact_ref, h_ref):
    @pl.when(pl.program_id(1) == 0)
    def _():
        h_ref[...] = _rms_rows(x_ref[...], g_ref[...]).astype(BF16)

    h = h_ref[...]
    q = jnp.dot(h, wq_ref[...], preferred_element_type=F32)
    q_ref[...] = (_rope(q, cos_ref, sina_ref, sinb_ref) * QK_SCALE).astype(BF16)
    k = _rope(jnp.dot(h, wk_ref[...], preferred_element_type=F32), cos_ref, sina_ref, sinb_ref)
    k_ref[...] = k
    kb_ref[...] = k.astype(BF16)
    v = jnp.dot(h, wv_ref[...], preferred_element_type=F32)
    v_ref[...] = v
    vb_ref[...] = v.astype(BF16)
    gt = jnp.dot(h, wg_ref[...], preferred_element_type=F32)
    gact_ref[...] = _silu(gt).astype(BF16)


def _inproj_attn(x2d, g, w, tables, *, tm, tn):
    m, d = x2d.shape
    a = w.shape[1] // 4
    nj = a // tn
    nper = tables[0].shape[0] // tm
    tab = pl.BlockSpec((tm, HEAD_DIM), lambda i, j: (i % nper, 0))
    out = pl.BlockSpec((tm, tn), lambda i, j: (i, j))
    wspec = lambda part: pl.BlockSpec((d, tn), lambda i, j: (0, j + part * nj))
    return pl.pallas_call(
        _inproj_attn_kernel,
        grid=(m // tm, nj),
        in_specs=[
            pl.BlockSpec((tm, d), lambda i, j: (i, 0)),
            pl.BlockSpec((1, d), lambda i, j: (0, 0)),
            wspec(0), wspec(1), wspec(2), wspec(3),
            tab, tab, tab,
        ],
        out_specs=[out] * 6,
        out_shape=[jax.ShapeDtypeStruct((m, a), dt) for dt in (BF16, F32, F32, BF16, BF16, BF16)],
        scratch_shapes=[pltpu.VMEM((tm, d), BF16)],
        compiler_params=_params(("parallel", "arbitrary")),
        name="inproj_attn",
    )(x2d, g, w, w, w, w, *tables)


def _rope_tables(pos, reps):
    half = ROT_DIM // 2
    inv_freq = ROPE_THETA ** (-jnp.arange(0, ROT_DIM, 2, dtype=F32) / ROT_DIM)
    ang = pos.astype(F32)[:, None] * inv_freq[None, :]
    cos, sin = jnp.cos(ang), jnp.sin(ang)
    t = pos.shape[0]
    cos_t = jnp.concatenate([cos, cos, jnp.ones((t, HEAD_DIM - ROT_DIM), F32)], axis=1)
    sina = jnp.concatenate([-sin, jnp.zeros((t, HEAD_DIM - half), F32)], axis=1)
    sinb = jnp.concatenate([jnp.zeros((t, half), F32), sin, jnp.zeros((t, HEAD_DIM - ROT_DIM), F32)], axis=1)
    return tuple(jnp.tile(x, (reps, 1)) for x in (cos_t, sina, sinb))


def _lambda_kernel(lq1_ref, lk1_ref, lq2_ref, lk2_ref, o_ref, *, lam_init):
    s1 = jnp.sum(lq1_ref[...] * lk1_ref[...], axis=-1, keepdims=True)
    s2 = jnp.sum(lq2_ref[...] * lk2_ref[...], axis=-1, keepdims=True)
    o_ref[...] = jnp.exp(s1) - jnp.exp(s2) + lam_init


def _diff_lambda(lq1, lk1, lq2, lk2, lam_init):
    return pl.pallas_call(
        functools.partial(_lambda_kernel, lam_init=lam_init),
        out_shape=jax.ShapeDtypeStruct((1, 1), F32),
        name="diff_lambda",
    )(lq1[None], lk1[None], lq2[None], lk2[None])


def _subln_gate(o, sg, gact, lam_init):
    o = o * lax.rsqrt(jnp.mean(o * o, axis=-1, keepdims=True) + SUBLN_EPS)
    o = o * sg * (1.0 - lam_init)
    return (o * gact.astype(F32)).astype(BF16)


_NT = (((1,), (1,)), ((), ()))


def _attn_prompt_kernel(lam_ref, q_ref, k_ref, v_ref, gact_ref, sg_ref, z_ref,
                        acc_ref, m_ref, l_ref, *, tq, lam_init):
    qi = pl.program_id(2)
    q = q_ref[0]
    qs = (q[:, :HEAD_DIM], q[:, HEAD_DIM:])
    m_ref[...] = jnp.full(m_ref.shape, -jnp.inf, F32)
    l_ref[...] = jnp.zeros(l_ref.shape, F32)
    acc_ref[...] = jnp.zeros(acc_ref.shape, F32)

    def block(k0, mask):
        kblk = k_ref[0, pl.ds(k0, tq), :]
        vblk = v_ref[0, pl.ds(k0, tq), :]
        for c in range(2):
            s = lax.dot_general(qs[c], kblk[:, c * HEAD_DIM:(c + 1) * HEAD_DIM], _NT,
                                preferred_element_type=F32)
            if mask is not None:
                s = jnp.where(mask, s, -jnp.inf)
            m_old = m_ref[c]
            m_new = jnp.maximum(m_old, jnp.max(s, axis=-1, keepdims=True))
            alpha = jnp.exp2(m_old - m_new)
            p = jnp.exp2(s - m_new)
            l_ref[c] = alpha * l_ref[c] + jnp.sum(p, axis=-1, keepdims=True)
            acc_ref[c] = alpha * acc_ref[c] + jnp.dot(p.astype(BF16), vblk, preferred_element_type=F32)
            m_ref[c] = m_new

    def body(kb, carry):
        block(pl.multiple_of(kb * tq, tq), None)
        return carry

    lax.fori_loop(0, qi, body, 0)
    rchunk = lax.broadcasted_iota(jnp.int32, (tq, tq), 0) // CHUNK
    cchunk = lax.broadcasted_iota(jnp.int32, (tq, tq), 1) // CHUNK
    block(pl.multiple_of(qi * tq, tq), cchunk <= rchunk)

    o = acc_ref[0] / l_ref[0] - lam_ref[0, 0] * (acc_ref[1] / l_ref[1])
    z_ref[0] = _subln_gate(o, sg_ref[...], gact_ref[0], lam_init)


def _attn_prompt(lam, q, kb, vb, gact, sg, *, tq, lam_init):
    n, s, a = q.shape
    nh = a // HEAD_W
    qspec = pl.BlockSpec((1, tq, HEAD_W), lambda b, h, i: (b, i, h))
    kspec = pl.BlockSpec((1, s, HEAD_W), lambda b, h, i: (b, 0, h))
    return pl.pallas_call(
        functools.partial(_attn_prompt_kernel, tq=tq, lam_init=lam_init),
        grid=(n, nh, s // tq),
        in_specs=[
            pl.BlockSpec(memory_space=pltpu.SMEM),
            qspec, kspec, kspec, qspec,
            pl.BlockSpec((1, HEAD_W), lambda b, h, i: (0, 0)),
        ],
        out_specs=qspec,
        out_shape=jax.ShapeDtypeStruct((n, s, a), BF16),
        scratch_shapes=[pltpu.VMEM((2, tq, HEAD_W), F32), pltpu.VMEM((2, tq, 1), F32),
                        pltpu.VMEM((2, tq, 1), F32)],
        compiler_params=_params(("parallel", "parallel", "arbitrary")),
        name="attn_prompt",
    )(lam, q, kb, vb, gact, sg)


def _attn_sample_kernel(lam_ref, q_ref, ck_ref, cv_ref, kn_ref, vn_ref, gact_ref, sg_ref, z_ref,
                        *, lam_init):
    q = q_ref[0]
    kc = ck_ref[0].astype(BF16)
    kn = kn_ref[0]
    weights = []
    for c in range(2):
        cols = slice(c * HEAD_DIM, (c + 1) * HEAD_DIM)
        sc = lax.dot_general(q[:, cols], kc[:, cols], _NT, preferred_element_type=F32)
        sn = lax.dot_general(q[:, cols], kn[:, cols], _NT, preferred_element_type=F32)
        m = jnp.maximum(jnp.max(sc, axis=-1, keepdims=True), jnp.max(sn, axis=-1, keepdims=True))
        pc, pn = jnp.exp2(sc - m), jnp.exp2(sn - m)
        l = jnp.sum(pc, axis=-1, keepdims=True) + jnp.sum(pn, axis=-1, keepdims=True)
        weights.append((pc / l, pn / l))
    lam = lam_ref[0, 0]
    ac = (weights[0][0] - lam * weights[1][0]).astype(BF16)
    an = (weights[0][1] - lam * weights[1][1]).astype(BF16)
    o = (jnp.dot(ac, cv_ref[0].astype(BF16), preferred_element_type=F32)
         + jnp.dot(an, vn_ref[0], preferred_element_type=F32))
    z_ref[0] = _subln_gate(o, sg_ref[...], gact_ref[0], lam_init)


def _attn_sample(lam, q, ck, cv, kn, vn, gact, sg, *, lam_init):
    n, t, a = q.shape
    past = ck.shape[1]
    nh = a // HEAD_W
    new = pl.BlockSpec((1, t, HEAD_W), lambda b, h: (b, 0, h))
    old = pl.BlockSpec((1, past, HEAD_W), lambda b, h: (b, 0, h))
    return pl.pallas_call(
        functools.partial(_attn_sample_kernel, lam_init=lam_init),
        grid=(n, nh),
        in_specs=[pl.BlockSpec(memory_space=pltpu.SMEM), new, old, old, new, new, new,
                  pl.BlockSpec((1, HEAD_W), lambda b, h: (0, 0))],
        out_specs=new,
        out_shape=jax.ShapeDtypeStruct((n, t, a), BF16),
        compiler_params=_params(("parallel", "parallel")),
        name="attn_sample",
    )(lam, q, ck, cv, kn, vn, gact, sg)


def _tile(dim, target):
    t = min(dim, target)
    assert dim % t == 0, (dim, target)
    return t


def kernel(x_prompt, x_sample, state_conv, cache_k, cache_v, norm_pre, norm_post, conv_w_in, conv_dw, conv_dw_b, conv_ln_g, conv_ln_b, conv_w_out, attn_w_in, attn_lq1, attn_lk1, attn_lq2, attn_lk2, attn_subln_g, attn_w_out):
    n_p, s_p, d = x_prompt.shape
    n_s, s_s, _ = x_sample.shape
    past = cache_k.shape[2]
    m_p, m_s = n_p * s_p, n_s * s_s
    row = lambda v: v[None, :]

    w_in0 = conv_w_in[0].astype(BF16)
    w_out0 = conv_w_out[0].astype(BF16)
    e = w_out0.shape[0]
    g_pre0, g_post0 = row(norm_pre[0]), row(norm_post[0])
    conv_args = (conv_dw[0], row(conv_dw_b[0]), row(conv_ln_g[0]), row(conv_ln_b[0]))

    xp2d = x_prompt.reshape(m_p, d)
    xs2d = x_sample.reshape(m_s, d)

    glu_p, gact_p = _inproj_conv(xp2d, g_pre0, w_in0, tm=_tile(m_p, 1024), tn=256)
    glu_s, gact_s = _inproj_conv(xs2d, g_pre0, w_in0, tm=_tile(m_s, 1024), tn=256)
    glu_p = glu_p.reshape(n_p, s_p, e)
    glu_s = glu_s.reshape(n_s, s_s, e)

    z_p = _convmix(glu_p, glu_p, gact_p.reshape(n_p, s_p, e), *conv_args,
                   tm=_tile(s_p, 256), zero_first=True)
    state_pad = jnp.pad(state_conv[0], ((0, 0), (STATE_ROWS - (CONV_WIDTH - 1), 0), (0, 0)))
    z_s = _convmix(glu_s, state_pad, gact_s.reshape(n_s, s_s, e), *conv_args,
                   tm=s_s, zero_first=False)

    y1_p = _outproj(z_p.reshape(m_p, e), w_out0, g_post0, xp2d, tm=_tile(m_p, 512))
    y1_s = _outproj(z_s.reshape(m_s, e), w_out0, g_post0, xs2d, tm=_tile(m_s, 512))

    keep = CONV_WIDTH - 1
    new_conv_prompt = glu_p[None, :, s_p - keep:, :]
    new_conv_sample = jnp.concatenate([state_conv[0], glu_s], axis=1)[None, :, -keep:, :]

    lam_init = 0.8 - 0.6 * math.exp(-0.3 * 1)
    w_in1 = attn_w_in[0].astype(BF16)
    w_out1 = attn_w_out[0].astype(BF16)
    a = w_out1.shape[0]
    g_pre1, g_post1 = row(norm_pre[1]), row(norm_post[1])
    sg = row(attn_subln_g[0])
    lam = _diff_lambda(attn_lq1[0], attn_lk1[0], attn_lq2[0], attn_lk2[0], lam_init)

    tm_p = _tile(s_p, 1024)
    q_p, k_p, v_p, kb_p, vb_p, ga_p = _inproj_attn(
        y1_p, g_pre1, w_in1, _rope_tables(jnp.arange(s_p), 1), tm=tm_p, tn=256)
    q_s, k_s, v_s, kb_s, vb_s, ga_s = _inproj_attn(
        y1_s, g_pre1, w_in1, _rope_tables(past + jnp.arange(s_s), n_s), tm=m_s, tn=256)

    as3 = lambda x, n, t: x.reshape(n, t, a)
    zat_p = _attn_prompt(lam, as3(q_p, n_p, s_p), as3(kb_p, n_p, s_p), as3(vb_p, n_p, s_p),
                         as3(ga_p, n_p, s_p), sg, tq=_tile(s_p, 512), lam_init=lam_init)
    zat_s = _attn_sample(lam, as3(q_s, n_s, s_s), cache_k[0].reshape(n_s, past, a),
                         cache_v[0].reshape(n_s, past, a), as3(kb_s, n_s, s_s), as3(vb_s, n_s, s_s),
                         as3(ga_s, n_s, s_s), sg, lam_init=lam_init)

    y2_p = _outproj(zat_p.reshape(m_p, a), w_out1, g_post1, y1_p, tm=_tile(m_p, 512))
    y2_s = _outproj(zat_s.reshape(m_s, a), w_out1, g_post1, y1_s, tm=_tile(m_s, 512))

    heads = lambda x, n, t: x.reshape(1, n, t, N_HEADS, HEAD_W)
    return (y2_p.reshape(n_p, s_p, d), y2_s.reshape(n_s, s_s, d),
            new_conv_prompt, new_conv_sample,
            heads(k_p, n_p, s_p), heads(v_p, n_p, s_p), heads(k_s, n_s, s_s), heads(v_s, n_s, s_s))
```

```python
import functools
import math

import jax
import jax.numpy as jnp
from jax import lax
from jax.experimental import pallas as pl
from jax.experimental.pallas import tpu as pltpu

CHUNK = 64
CONV_WIDTH = 31
N_HEADS = 8
HEAD_DIM = 128
HEAD_W = 2 * HEAD_DIM
ROT_DIM = HEAD_DIM // 4
ROPE_THETA = 500000.0
NORM_EPS = 1e-6
SUBLN_EPS = 1e-5
LN_EPS = 1e-5

F32 = jnp.float32
BF16 = jnp.bfloat16

VMEM_LIMIT = 56 * 1024 * 1024
STATE_ROWS = 32
LANES = 128
CONV_ROWS = 128
CONV_LANES = 128
LN_ROWS = 16
ATTN_ROWS = 128
QK_SCALE = HEAD_DIM ** -0.5 * math.log2(math.e)


def _params(sem, flags=None):
    return pltpu.CompilerParams(dimension_semantics=sem, vmem_limit_bytes=VMEM_LIMIT, flags=flags)


def _rms_rows(x, g):
    ms = jnp.mean(x * x, axis=-1, keepdims=True)
    return x * lax.rsqrt(ms + NORM_EPS) * g


def _silu(x):
    return x * jax.nn.sigmoid(x)


def _inproj_conv_kernel(x_ref, g_ref, wa_ref, wb_ref, wg_ref, glu_ref, gact_ref, h_ref):
    @pl.when(pl.program_id(1) == 0)
    def _():
        h_ref[...] = _rms_rows(x_ref[...], g_ref[...]).astype(BF16)

    h = h_ref[...]
    a = jnp.dot(h, wa_ref[...], preferred_element_type=F32)
    b = jnp.dot(h, wb_ref[...], preferred_element_type=F32)
    gt = jnp.dot(h, wg_ref[...], preferred_element_type=F32)
    glu_ref[...] = a * jax.nn.sigmoid(b)
    gact_ref[...] = _silu(gt).astype(BF16)


def _inproj_conv(x2d, g, w, *, tm, tn):
    m, d = x2d.shape
    e = w.shape[1] // 3
    nj = e // tn
    row = lambda i, j: (i, 0)
    return pl.pallas_call(
        _inproj_conv_kernel,
        grid=(m // tm, nj),
        in_specs=[
            pl.BlockSpec((tm, d), row),
            pl.BlockSpec((1, d), lambda i, j: (0, 0)),
            pl.BlockSpec((d, tn), lambda i, j: (0, j)),
            pl.BlockSpec((d, tn), lambda i, j: (0, j + nj)),
            pl.BlockSpec((d, tn), lambda i, j: (0, j + 2 * nj)),
        ],
        out_specs=[pl.BlockSpec((tm, tn), lambda i, j: (i, j)),
                   pl.BlockSpec((tm, tn), lambda i, j: (i, j))],
        out_shape=[jax.ShapeDtypeStruct((m, e), F32), jax.ShapeDtypeStruct((m, e), BF16)],
        scratch_shapes=[pltpu.VMEM((tm, d), BF16)],
        compiler_params=_params(("parallel", "arbitrary")),
        name="inproj_conv",
    )(x2d, g, w, w, w)


def _conv_taps():
    lead = STATE_ROWS - (CONV_WIDTH - 1)
    groups = {}
    for w in range(CONV_WIDTH):
        o = w + lead
        start = o % 8 if o % 8 >= lead else o % 8 + 8
        groups.setdefault(start, []).append((w, o - start))
    return sorted(groups.items())


def _convmix_kernel(cur_ref, prev_ref, gact_ref, dw_ref, dwb_ref, lng_ref, lnb_ref, z_ref,
                    xp_ref, y_ref, sum_ref, mu_ref, rstd_ref, *, tm, zero_first):
    e = cur_ref.shape[-1]
    prev = prev_ref[0]
    if zero_first:
        prev = jnp.where(pl.program_id(1) == 0, 0.0, prev)
    xp_ref[0:STATE_ROWS, :] = prev
    xp_ref[STATE_ROWS:, :] = cur_ref[0]

    rows = min(CONV_ROWS, tm)
    taps = _conv_taps()
    lane_tiles = lambda x: [x[:, t * LANES:(t + 1) * LANES] for t in range(x.shape[1] // LANES)]
    sum_ref[...] = jnp.zeros(sum_ref.shape, F32)

    def lane_chunk(c, carry):
        c0 = pl.multiple_of(c * CONV_LANES, CONV_LANES)
        lanes = pl.ds(c0, CONV_LANES)
        for r0 in range(0, tm, rows):
            acc = jnp.broadcast_to(dwb_ref[:, lanes], (rows, CONV_LANES))
            for start, group in taps:
                span = rows + max(off for _, off in group)
                shift = start % 8
                if shift:
                    win = xp_ref[pl.ds(r0 + start - shift, span + 8), lanes]
                    win = pltpu.roll(win, span + 8 - shift, axis=0)
                else:
                    win = xp_ref[pl.ds(r0 + start, span), lanes]
                for w, off in group:
                    acc = acc + win[off:off + rows, :] * dw_ref[pl.ds(w, 1), lanes]
            y_ref[pl.ds(r0, rows), lanes] = acc
            sum_ref[pl.ds(r0, rows), :] += functools.reduce(jnp.add, lane_tiles(acc))
        return carry

    lax.fori_loop(0, e // CONV_LANES, lane_chunk, 0)

    ln_rows = min(LN_ROWS, tm)
    mu_ref[...] = jnp.broadcast_to(jnp.sum(sum_ref[...], axis=-1, keepdims=True) * (1.0 / e), mu_ref.shape)

    def centered_squares(i, carry):
        rs = pl.ds(pl.multiple_of(i * ln_rows, ln_rows), ln_rows)
        mu = mu_ref[rs, :]
        sq = [(t - mu) * (t - mu) for t in lane_tiles(y_ref[rs, :])]
        sum_ref[rs, :] = functools.reduce(jnp.add, sq)
        return carry

    lax.fori_loop(0, tm // ln_rows, centered_squares, 0)
    var = jnp.sum(sum_ref[...], axis=-1, keepdims=True) * (1.0 / e)
    rstd_ref[...] = jnp.broadcast_to(lax.rsqrt(var + LN_EPS), rstd_ref.shape)

    def normalize(i, carry):
        rs = pl.ds(pl.multiple_of(i * ln_rows, ln_rows), ln_rows)
        mu, rstd = mu_ref[rs, :], rstd_ref[rs, :]
        outs = []
        for t, y in enumerate(lane_tiles(y_ref[rs, :])):
            cols = slice(t * LANES, (t + 1) * LANES)
            yn =(y - mu) * rstd * lng_ref[:, cols] + lnb_ref[:, cols]
            outs.append(_silu(yn) * gact_ref[0, rs, cols].astype(F32))
        z_ref[0, rs, :] = jnp.concatenate(outs, axis=1).astype(BF16)
        return carry

    lax.fori_loop(0, tm // ln_rows, normalize, 0)


def _convmix(glu, prev, gact, dw, dwb, lng, lnb, *, tm, zero_first):
    n, t, e = glu.shape
    per = tm // STATE_ROWS
    if zero_first:
        prev_map = lambda b, i: (b, jnp.maximum(i * per - 1, 0), 0)
    else:
        prev_map = lambda b, i: (b, 0, 0)
    vec = pl.BlockSpec((1, e), lambda b, i: (0, 0))
    return pl.pallas_call(
        functools.partial(_convmix_kernel, tm=tm, zero_first=zero_first),
        grid=(n, t // tm),
        in_specs=[
            pl.BlockSpec((1, tm, e), lambda b, i: (b, i, 0)),
            pl.BlockSpec((1, STATE_ROWS, e), prev_map),
            pl.BlockSpec((1, tm, e), lambda b, i: (b, i, 0)),
            pl.BlockSpec((CONV_WIDTH, e), lambda b, i: (0, 0)),
            vec, vec, vec,
        ],
        out_specs=pl.BlockSpec((1, tm, e), lambda b, i: (b, i, 0)),
        out_shape=jax.ShapeDtypeStruct((n, t, e), BF16),
        scratch_shapes=[pltpu.VMEM((tm + STATE_ROWS, e), F32), pltpu.VMEM((tm, e), F32)]
        + [pltpu.VMEM((tm, LANES), F32)] * 3,
        compiler_params=_params(("parallel", "arbitrary")),
        name="convmix",
    )(glu, prev, gact, dw, dwb, lng, lnb)


def _outproj_kernel(z_ref, w_ref, g_ref, res_ref, o_ref):
    u = jnp.dot(z_ref[...], w_ref[...], preferred_element_type=F32)
    o_ref[...] = res_ref[...] + _rms_rows(u, g_ref[...])


def _outproj(z2d, w, g, res2d, *, tm):
    m, k = z2d.shape
    d = w.shape[1]
    return pl.pallas_call(
        _outproj_kernel,
        grid=(m // tm,),
        in_specs=[
            pl.BlockSpec((tm, k), lambda i: (i, 0)),
            pl.BlockSpec((k, d), lambda i: (0, 0)),
            pl.BlockSpec((1, d), lambda i: (0, 0)),
            pl.BlockSpec((tm, d), lambda i: (i, 0)),
        ],
        out_specs=pl.BlockSpec((tm, d), lambda i: (i, 0)),
        out_shape=jax.ShapeDtypeStruct((m, d), F32),
        compiler_params=_params(("parallel",)),
        name="outproj",
    )(z2d, w, g, res2d)


def _rope(x, cos_ref, sina_ref, sinb_ref):
    cos, sina, sinb = cos_ref[...], sina_ref[...], sinb_ref[...]
    half = ROT_DIM // 2
    outs = []
    for c in range(x.shape[1] // HEAD_DIM):
        xc = x[:, c * HEAD_DIM:(c + 1) * HEAD_DIM]
        up = pltpu.roll(xc, HEAD_DIM - half, axis=1)
        dn = pltpu.roll(xc, half, axis=1)
        outs.append(xc * cos + up * sina + dn * sinb)
    return jnp.concatenate(outs, axis=1)


def _inproj_attn_kernel(x_ref, g_ref, wq_ref, wk_ref, wv_ref, wg_ref, cos_ref, sina_ref, sinb_ref,
                        q_ref, k_ref, v_ref, kb_ref, vb_ref, gact_ref, h_ref):
    @pl.when(pl.program_id(1) == 0)
    def _():
        h_ref[...] = _rms_rows(x_ref[...], g_ref[...]).astype(BF16)

    h = h_ref[...]
    q = jnp.dot(h, wq_ref[...], preferred_element_type=F32)
    q_ref[...] = (_rope(q, cos_ref, sina_ref, sinb_ref) * QK_SCALE).astype(BF16)
    k = _rope(jnp.dot(h, wk_ref[...], preferred_element_type=F32), cos_ref, sina_ref, sinb_ref)
    k_ref[...] = k
    kb_ref[...] = k.astype(BF16)
    v = jnp.dot(h, wv_ref[...], preferred_element_type=F32)
    v_ref[...] = v
    vb_ref[...] = v.astype(BF16)
    gt = jnp.dot(h, wg_ref[...], preferred_element_type=F32)
    gact_ref[...] = _silu(gt).astype(BF16)


def _inproj_attn(x2d, g, w, tables, *, tm, tn):
    m, d = x2d.shape
    a = w.shape[1] // 4
    nj = a // tn
    nper = tables[0].shape[0] // tm
    tab = pl.BlockSpec((tm, HEAD_DIM), lambda i, j: (i % nper, 0))
    out = pl.BlockSpec((tm, tn), lambda i, j: (i, j))
    wspec = lambda part: pl.BlockSpec((d, tn), lambda i, j: (0, j + part * nj))
    return pl.pallas_call(
        _inproj_attn_kernel,
        grid=(m // tm, nj),
        in_specs=[
            pl.BlockSpec((tm, d), lambda i, j: (i, 0)),
            pl.BlockSpec((1, d), lambda i, j: (0, 0)),
            wspec(0), wspec(1), wspec(2), wspec(3),
            tab, tab, tab,
        ],
        out_specs=[out] * 6,
        out_shape=[jax.ShapeDtypeStruct((m, a), dt) for dt in (BF16, F32, F32, BF16, BF16, BF16)],
        scratch_shapes=[pltpu.VMEM((tm, d), BF16)],
        compiler_params=_params(("parallel", "arbitrary")),
        name="inproj_attn",
    )(x2d, g, w, w, w, w, *tables)


def _rope_tables(pos, reps):
    half = ROT_DIM // 2
    inv_freq = ROPE_THETA ** (-jnp.arange(0, ROT_DIM, 2, dtype=F32) / ROT_DIM)
    ang = pos.astype(F32)[:, None] * inv_freq[None, :]
    cos, sin = jnp.cos(ang), jnp.sin(ang)
    t = pos.shape[0]
    cos_t = jnp.concatenate([cos, cos, jnp.ones((t, HEAD_DIM - ROT_DIM), F32)], axis=1)
    sina = jnp.concatenate([-sin, jnp.zeros((t, HEAD_DIM - half), F32)], axis=1)
    sinb = jnp.concatenate([jnp.zeros((t, half), F32), sin, jnp.zeros((t, HEAD_DIM - ROT_DIM), F32)], axis=1)
    return tuple(jnp.tile(x, (reps, 1)) for x in (cos_t, sina, sinb))


def _lambda_kernel(lq1_ref, lk1_ref, lq2_ref, lk2_ref, o_ref, *, lam_init):
    s1 = jnp.sum(lq1_ref[...] * lk1_ref[...], axis=-1, keepdims=True)
    s2 = jnp.sum(lq2_ref[...] * lk2_ref[...], axis=-1, keepdims=True)
    o_ref[...] = jnp.exp(s1) - jnp.exp(s2) + lam_init


def _diff_lambda(lq1, lk1, lq2, lk2, lam_init):
    return pl.pallas_call(
        functools.partial(_lambda_kernel, lam_init=lam_init),
        out_shape=jax.ShapeDtypeStruct((1, 1), F32),
        name="diff_lambda",
    )(lq1[None], lk1[None], lq2[None], lk2[None])


def _subln_gate(o, sg, gact, lam_init):
    o = o * lax.rsqrt(jnp.mean(o * o, axis=-1, keepdims=True) + SUBLN_EPS)
    o = o * sg * (1.0 - lam_init)
    return (o * gact.astype(F32)).astype(BF16)


_NT = (((1,), (1,)), ((), ()))


def _attn_prompt_kernel(lam_ref, q_ref, k_ref, v_ref, gact_ref, sg_ref, z_ref,
                        *scratch, tq, lam_init):
    tk = tq
    n = pl.program_id(2)
    acc_ref, m_ref, l_ref = scratch[0:2], scratch[2:4], scratch[4:6]
    al_ref, p_ref, s_ref = scratch[6:10], scratch[10:14], scratch[14:18]
    for c in range(2):
        m_ref[c][...] = jnp.full(m_ref[c].shape, -jnp.inf, F32)
        l_ref[c][...] = jnp.zeros(l_ref[c].shape, F32)
        acc_ref[c][...] = jnp.zeros(acc_ref[c].shape, F32)

    def keys(ref, j):
        return ref[0, pl.ds(pl.multiple_of(j * tk, tk), tk), :]

    def scores(j, slot):
        kblk = keys(k_ref, j)
        for c in range(2):
            cols = slice(c * HEAD_DIM, (c + 1) * HEAD_DIM)
            s_ref[2 * slot + c][...] = lax.dot_general(
                q_ref[0, :, cols], kblk[:, cols], _NT, preferred_element_type=F32)

    def softmax(slot, masked=False):
        for c in range(2):
            b = 2 * slot + c
            for r0 in range(0, tq, ATTN_ROWS):
                rows = pl.ds(r0, ATTN_ROWS)
                tiles = [s_ref[b][rows, t * HEAD_DIM:(t + 1) * HEAD_DIM] for t in range(tk // HEAD_DIM)]
                if masked:
                    rchunk = (r0 + lax.broadcasted_iota(jnp.int32, (ATTN_ROWS, HEAD_DIM), 0)) // CHUNK
                    cchunk = lax.broadcasted_iota(jnp.int32, (ATTN_ROWS, HEAD_DIM), 1) // CHUNK
                    tiles = [jnp.where(cchunk + t * (HEAD_DIM // CHUNK) <= rchunk, x, -jnp.inf)
                             for t, x in enumerate(tiles)]
                m_old = m_ref[c][rows, :]
                m_new = jnp.maximum(
                    m_old, jnp.max(functools.reduce(jnp.maximum, tiles), axis=-1, keepdims=True))
                alpha = jnp.exp2(m_old - m_new)
                ps = [jnp.exp2(x - m_new) for x in tiles]
                l_ref[c][rows, :] = alpha * l_ref[c][rows, :] + functools.reduce(jnp.add, ps)
                m_ref[c][rows, :] = m_new
                al_ref[b][rows, :] = alpha
                p_ref[b][rows, :] = jnp.concatenate(ps, axis=1).astype(BF16)

    def values(j, slot):
        vblk = keys(v_ref, j)
        for c in range(2):
            b = 2 * slot + c
            pv = jnp.dot(p_ref[b][...], vblk, preferred_element_type=F32)
            alpha = al_ref[b][...]
            acc_ref[c][...] = jnp.concatenate([alpha, alpha], axis=1) * acc_ref[c][...] + pv

    def step(j, *, s=True, f="plain", v=True):
        for slot in range(2):
            @pl.when(j % 2 == slot)
            def _():
                if s:
                    scores(j + 1, 1 - slot)
                if f is not None:
                    softmax(slot, masked=(f == "masked"))
                if v:
                    values(j - 1, 1 - slot)

    scores(0, 0)

    @pl.when(n == 0)
    def _():
        softmax(0, masked=True)

    @pl.when(n > 0)
    def _():
        step(0, v=False)

    def body(j, carry):
        step(j)
        return carry

    lax.fori_loop(1, n, body, 0)

    @pl.when(n > 0)
    def _():
        step(n, s=False, f="masked")

    step(n + 1, s=False, f=None)

    outs = [acc_ref[c][...] / jnp.sum(l_ref[c][...], axis=-1, keepdims=True) for c in range(2)]
    o = outs[0] - lam_ref[0, 0] * outs[1]
    z_ref[0] = _subln_gate(o, sg_ref[...], gact_ref[0], lam_init)


def _attn_prompt(lam, q, kb, vb, gact, sg, *, tq, lam_init):
    n, s, a = q.shape
    nh = a // HEAD_W
    qspec = pl.BlockSpec((1, tq, HEAD_W), lambda b, h, i: (b, i, h))
    kspec = pl.BlockSpec((1, s, HEAD_W), lambda b, h, i: (b, 0, h))
    scratch = ([pltpu.VMEM((tq, HEAD_W), F32)] * 2
               + [pltpu.VMEM((tq, HEAD_DIM), F32)] * 4
               + [pltpu.VMEM((tq, HEAD_DIM), F32)] * 4
               + [pltpu.VMEM((tq, tq), BF16)] * 4
               + [pltpu.VMEM((tq, tq), F32)] * 4)
    return pl.pallas_call(
        functools.partial(_attn_prompt_kernel, tq=tq, lam_init=lam_init),
        grid=(n, nh, s // tq),
        in_specs=[
            pl.BlockSpec(memory_space=pltpu.SMEM),
            qspec, kspec, kspec, qspec,
            pl.BlockSpec((1, HEAD_W), lambda b, h, i: (0, 0)),
        ],
        out_specs=qspec,
        out_shape=jax.ShapeDtypeStruct((n, s, a), BF16),
        scratch_shapes=scratch,
        compiler_params=_params(("parallel", "parallel", "arbitrary")),
        name="attn_prompt",
    )(lam, q, kb, vb, gact, sg)


def _attn_sample_kernel(lam_ref, q_ref, ck_ref, cv_ref, kn_ref, vn_ref, gact_ref, sg_ref, z_ref,
                        *, lam_init):
    q = q_ref[0]
    kc = ck_ref[0].astype(BF16)
    kn = kn_ref[0]
    weights = []
    for c in range(2):
        cols = slice(c * HEAD_DIM, (c + 1) * HEAD_DIM)
        sc = lax.dot_general(q[:, cols], kc[:, cols], _NT, preferred_element_type=F32)
        sn = lax.dot_general(q[:, cols], kn[:, cols], _NT, preferred_element_type=F32)
        m = jnp.maximum(jnp.max(sc, axis=-1, keepdims=True), jnp.max(sn, axis=-1, keepdims=True))
        pc, pn = jnp.exp2(sc - m), jnp.exp2(sn - m)
        l = jnp.sum(pc, axis=-1, keepdims=True) + jnp.sum(pn, axis=-1, keepdims=True)
        weights.append((pc / l, pn / l))
    lam = lam_ref[0, 0]
    ac = (weights[0][0] - lam * weights[1][0]).astype(BF16)
    an = (weights[0][1] - lam * weights[1][1]).astype(BF16)
    o = (jnp.dot(ac, cv_ref[0].astype(BF16), preferred_element_type=F32)
         + jnp.dot(an, vn_ref[0], preferred_element_type=F32))
    z_ref[0] = _subln_gate(o, sg_ref[...], gact_ref[0], lam_init)


def _attn_sample(lam, q, ck, cv, kn, vn, gact, sg, *, lam_init):
    n, t, a = q.shape
    past = ck.shape[1]
    nh = a // HEAD_W
    new = pl.BlockSpec((1, t, HEAD_W), lambda b, h: (b, 0, h))
    old = pl.BlockSpec((1, past, HEAD_W), lambda b, h: (b, 0, h))
    return pl.pallas_call(
        functools.partial(_attn_sample_kernel, lam_init=lam_init),
        grid=(n, nh),
        in_specs=[pl.BlockSpec(memory_space=pltpu.SMEM), new, old, old, new, new, new,
                  pl.BlockSpec((1, HEAD_W), lambda b, h: (0, 0))],
        out_specs=new,
        out_shape=jax.ShapeDtypeStruct((n, t, a), BF16),
        compiler_params=_params(("parallel", "parallel")),
        name="attn_sample",
    )(lam, q, ck, cv, kn, vn, gact, sg)


def _tile(dim, target):
    t = min(dim, target)
    assert dim % t == 0, (dim, target)
    return t


def kernel(x_prompt, x_sample, state_conv, cache_k, cache_v, norm_pre, norm_post, conv_w_in, conv_dw, conv_dw_b, conv_ln_g, conv_ln_b, conv_w_out, attn_w_in, attn_lq1, attn_lk1, attn_lq2, attn_lk2, attn_subln_g, attn_w_out):
    n_p, s_p, d = x_prompt.shape
    n_s, s_s, _ = x_sample.shape
    past = cache_k.shape[2]
    m_p, m_s = n_p * s_p, n_s * s_s
    row = lambda v: v[None, :]

    w_in0 = conv_w_in[0].astype(BF16)
    w_out0 = conv_w_out[0].astype(BF16)
    e = w_out0.shape[0]
    g_pre0, g_post0 = row(norm_pre[0]), row(norm_post[0])
    conv_args = (conv_dw[0], row(conv_dw_b[0]), row(conv_ln_g[0]), row(conv_ln_b[0]))

    xp2d = x_prompt.reshape(m_p, d)
    xs2d = x_sample.reshape(m_s, d)

    glu_p, gact_p = _inproj_conv(xp2d, g_pre0, w_in0, tm=_tile(m_p, 1024), tn=256)
    glu_s, gact_s = _inproj_conv(xs2d, g_pre0, w_in0, tm=_tile(m_s, 1024), tn=256)
    glu_p = glu_p.reshape(n_p, s_p, e)
    glu_s = glu_s.reshape(n_s, s_s, e)

    z_p = _convmix(glu_p, glu_p, gact_p.reshape(n_p, s_p, e), *conv_args,
                   tm=_tile(s_p, 256), zero_first=True)
    state_pad = jnp.pad(state_conv[0], ((0, 0), (STATE_ROWS - (CONV_WIDTH - 1), 0), (0, 0)))
    z_s = _convmix(glu_s, state_pad, gact_s.reshape(n_s, s_s, e), *conv_args,
                   tm=s_s, zero_first=False)

    y1_p = _outproj(z_p.reshape(m_p, e), w_out0, g_post0, xp2d, tm=_tile(m_p, 512))
    y1_s = _outproj(z_s.reshape(m_s, e), w_out0, g_post0, xs2d, tm=_tile(m_s, 512))

    keep = CONV_WIDTH - 1
    new_conv_prompt = glu_p[None, :, s_p - keep:, :]
    new_conv_sample = jnp.concatenate([state_conv[0], glu_s], axis=1)[None, :, -keep:, :]

    lam_init = 0.8 - 0.6 * math.exp(-0.3 * 1)
    w_in1 = attn_w_in[0].astype(BF16)
    w_out1 = attn_w_out[0].astype(BF16)
    a = w_out1.shape[0]
    g_pre1, g_post1 = row(norm_pre[1]), row(norm_post[1])
    sg = row(attn_subln_g[0])
    lam = _diff_lambda(attn_lq1[0], attn_lk1[0], attn_lq2[0], attn_lk2[0], lam_init)

    tm_p = _tile(s_p, 1024)
    q_p, k_p, v_p, kb_p, vb_p, ga_p = _inproj_attn(
        y1_p, g_pre1, w_in1, _rope_tables(jnp.arange(s_p), 1), tm=tm_p, tn=256)
    q_s, k_s, v_s, kb_s, vb_s, ga_s = _inproj_attn(
        y1_s, g_pre1, w_in1, _rope_tables(past + jnp.arange(s_s), n_s), tm=m_s, tn=256)

    as3 = lambda x, n, t: x.reshape(n, t, a)
    zat_p = _attn_prompt(lam, as3(q_p, n_p, s_p), as3(kb_p, n_p, s_p), as3(vb_p, n_p, s_p),
                         as3(ga_p, n_p, s_p), sg, tq=_tile(s_p, 512), lam_init=lam_init)
    zat_s = _attn_sample(lam, as3(q_s, n_s, s_s), cache_k[0].reshape(n_s, past, a),
                         cache_v[0].reshape(n_s, past, a), as3(kb_s, n_s, s_s), as3(vb_s, n_s, s_s),
                         as3(ga_s, n_s, s_s), sg, lam_init=lam_init)

    y2_p = _outproj(zat_p.reshape(m_p, a), w_out1, g_post1, y1_p, tm=_tile(m_p, 512))
    y2_s = _outproj(zat_s.reshape(m_s, a), w_out1, g_post1, y1_s, tm=_tile(m_s, 512))

    heads = lambda x, n, t: x.reshape(1, n, t, N_HEADS, HEAD_W)
    return (y2_p.reshape(n_p, s_p, d), y2_s.reshape(n_s, s_s, d),
            new_conv_prompt, new_conv_sample,
            heads(k_p, n_p, s_p), heads(v_p, n_p, s_p), heads(k_s, n_s, s_s), heads(v_s, n_s, s_s))
```
